```python
import jax
import jax.numpy as jnp
from jax import lax
import numpy as np

D_MODEL = 4096
BATCH = 4
SEQ = 2048
DEPTH = 4

W_A = D_MODEL // 4
W_B = D_MODEL // 2
W_C = D_MODEL // 4
D_MIX = W_A + W_B + W_C
HEAD_DIM = 64
N_Q_HEADS = W_B // HEAD_DIM
KV_GROUP = 8
N_KV_HEADS = N_Q_HEADS // KV_GROUP
KV_W = N_KV_HEADS * HEAD_DIM
WINDOW = 128
BLOCK = 128
CONV_A = 3
CONV_C = 4
N_RG_HEADS = 8
RG_BLOCK = W_C // N_RG_HEADS
RG_C = 8.0
IN_W = 4 * W_A + (W_B + 2 * KV_W + W_B) + 2 * W_C
DEEPNORM_ALPHA = (2.0 * DEPTH) ** 0.25
DEEPNORM_BETA = (8.0 * DEPTH) ** -0.25
LN_EPS = 1e-5
RMS_EPS = 1e-6
NEG_INF = -1e30

kernel_name = "hybrid_shortconv_swa_rglru_deepnorm"


def layer_norm(x, g, b):
    xf = x.astype(jnp.float32)
    mu = xf.mean(-1, keepdims=True)
    var = jnp.mean(jnp.square(xf - mu), -1, keepdims=True)
    y = (xf - mu) * lax.rsqrt(var + LN_EPS) * g.astype(jnp.float32) + b.astype(jnp.float32)
    return y.astype(x.dtype)


def rms_norm(x, g):
    xf = x.astype(jnp.float32)
    y = xf * lax.rsqrt(jnp.mean(xf * xf, -1, keepdims=True) + RMS_EPS) * g.astype(jnp.float32)
    return y.astype(x.dtype)


def causal_depthwise_conv(u, w):
    K = w.shape[0]
    S = u.shape[1]
    up = jnp.pad(u, ((0, 0), (K - 1, 0), (0, 0)))
    y = up[:, 0:S] * w[0]
    for k in range(1, K):
        y = y + up[:, k:k + S] * w[k]
    return y


def sliding_window_attention(q, k, v, sinks):
    B, S = q.shape[0], q.shape[1]
    nb = S // BLOCK
    qb = q.reshape(B, nb, BLOCK, N_KV_HEADS, KV_GROUP, HEAD_DIM)

    def with_prev(t):
        tb = t.reshape(B, nb, BLOCK, N_KV_HEADS, HEAD_DIM)
        prev = jnp.pad(tb, ((0, 0), (1, 0), (0, 0), (0, 0), (0, 0)))[:, :-1]
        return jnp.concatenate([prev, tb], axis=2)

    kk = with_prev(k)
    vv = with_prev(v)
    s = jnp.einsum("bnqhgd,bnkhd->bnhgqk", qb, kk).astype(jnp.float32) * (HEAD_DIM ** -0.5)
    qi = jnp.arange(BLOCK)[:, None]
    kj = jnp.arange(2 * BLOCK)[None, :]
    dist = qi + BLOCK - kj
    band = (dist >= 0) & (dist < WINDOW)
    blk = jnp.arange(nb)[:, None, None]
    valid = band[None] & ((blk > 0) | (kj[None] >= BLOCK))
    s = jnp.where(valid[None, :, None, None], s, NEG_INF)
    sink = sinks.astype(jnp.float32).reshape(N_KV_HEADS, KV_GROUP)[None, None, :, :, None, None]
    m = jnp.maximum(s.max(-1, keepdims=True), sink)
    p = jnp.exp(s - m)
    p = p / (p.sum(-1, keepdims=True) + jnp.exp(sink - m))
    o = jnp.einsum("bnhgqk,bnkhd->bnqhgd", p.astype(v.dtype), vv)
    return o.reshape(B, S, N_Q_HEADS * HEAD_DIM)


def rg_lru(xc, w_r, b_r, w_i, b_i, lam):
    B, S, _ = xc.shape
    xh = xc.reshape(B, S, N_RG_HEADS, RG_BLOCK)
    r = jax.nn.sigmoid(jnp.einsum("bshi,hij->bshj", xh, w_r).reshape(B, S, W_C) + b_r)
    i = jax.nn.sigmoid(jnp.einsum("bshi,hij->bshj", xh, w_i).reshape(B, S, W_C) + b_i)
    log_a = RG_C * r.astype(jnp.float32) * jax.nn.log_sigmoid(lam.astype(jnp.float32))
    a = jnp.exp(log_a)
    u = jnp.sqrt(-jnp.expm1(2.0 * log_a)) * (i * xc).astype(jnp.float32)

    def combine(c1, c2):
        a1, b1 = c1
        a2, b2 = c2
        return a1 * a2, a2 * b1 + b2

    _, hs = lax.associative_scan(combine, (a, u), axis=1)
    return hs.astype(xc.dtype)


def hybrid_layer(x, w_in, conv_a_w, sinks, conv_c_w, conv_c_b, gate_r_w, gate_r_b,
                 gate_i_w, gate_i_b, rg_lambda, norm_a, norm_b, norm_c, w_out, ln_g, ln_b):
    B, S, _ = x.shape
    h = jnp.einsum("bsd,de->bse", x, w_in)
    sizes = [W_A, W_A, W_A, W_A, W_B, KV_W, KV_W, W_B, W_C, W_C]
    offs = np.cumsum(sizes)[:-1].tolist()
    a_b, a_c, a_x, a_g, q, k, v, b_g, c_x, c_g = jnp.split(h, offs, axis=-1)
    y_a = a_b * causal_depthwise_conv(a_c * a_x, conv_a_w)
    y_b = sliding_window_attention(q.reshape(B, S, N_Q_HEADS, HEAD_DIM),
                                   k.reshape(B, S, N_KV_HEADS, HEAD_DIM),
                                   v.reshape(B, S, N_KV_HEADS, HEAD_DIM), sinks)
    xc = causal_depthwise_conv(c_x, conv_c_w) + conv_c_b
    y_c = rg_lru(xc, gate_r_w, gate_r_b, gate_i_w, gate_i_b, rg_lambda)
    mix = jnp.concatenate([rms_norm(y_a, norm_a) * jax.nn.silu(a_g),
                           rms_norm(y_b, norm_b) * jax.nn.silu(b_g),
                           rms_norm(y_c, norm_c) * jax.nn.silu(c_g)], axis=-1)
    out = jnp.einsum("bse,ed->bsd", mix, w_out)
    return layer_norm(DEEPNORM_ALPHA * x + out, ln_g, ln_b)


def setup_inputs(seed: int = 0) -> dict:
    key = jax.random.key(seed)
    ks = jax.random.split(key, 20)
    f32 = jnp.float32
    x = jax.random.normal(ks[0], (BATCH, SEQ, D_MODEL), f32)
    w_in = jax.random.normal(ks[1], (DEPTH, D_MODEL, IN_W), f32) * D_MODEL ** -0.5
    conv_a_w = jax.random.normal(ks[2], (DEPTH, CONV_A, W_A), f32) * CONV_A ** -0.5
    sinks = jax.random.normal(ks[3], (DEPTH, N_Q_HEADS), f32) * 0.5
    conv_c_w = jax.random.normal(ks[4], (DEPTH, CONV_C, W_C), f32) * CONV_C ** -0.5
    conv_c_b = jax.random.normal(ks[5], (DEPTH, W_C), f32) * 0.01
    gate_r_w = jax.random.normal(ks[6], (DEPTH, N_RG_HEADS, RG_BLOCK, RG_BLOCK), f32) * RG_BLOCK ** -0.5
    gate_r_b = jax.random.normal(ks[7], (DEPTH, W_C), f32) * 0.01
    gate_i_w = jax.random.normal(ks[8], (DEPTH, N_RG_HEADS, RG_BLOCK, RG_BLOCK), f32) * RG_BLOCK ** -0.5
    gate_i_b = jax.random.normal(ks[9], (DEPTH, W_C), f32) * 0.01
    a_pow_c = jax.random.uniform(ks[10], (DEPTH, W_C), f32, minval=0.9, maxval=0.999)
    a0 = a_pow_c ** (1.0 / RG_C)
    rg_lambda = jnp.log(a0) - jnp.log1p(-a0)
    norm_a = 1.0 + 0.01 * jax.random.normal(ks[11], (DEPTH, W_A), f32)
    norm_b = 1.0 + 0.01 * jax.random.normal(ks[12], (DEPTH, W_B), f32)
    norm_c = 1.0 + 0.01 * jax.random.normal(ks[13], (DEPTH, W_C), f32)
    w_out = jax.random.normal(ks[14], (DEPTH, D_MIX, D_MODEL), f32) * (D_MIX ** -0.5) * DEEPNORM_BETA
    ln_g = 1.0 + 0.01 * jax.random.normal(ks[15], (DEPTH, D_MODEL), f32)
    ln_b = 0.01 * jax.random.normal(ks[16], (DEPTH, D_MODEL), f32)
    return {"x": x, "w_in": w_in, "conv_a_w": conv_a_w, "sinks": sinks,
            "conv_c_w": conv_c_w, "conv_c_b": conv_c_b,
            "gate_r_w": gate_r_w, "gate_r_b": gate_r_b,
            "gate_i_w": gate_i_w, "gate_i_b": gate_i_b, "rg_lambda": rg_lambda,
            "norm_a": norm_a, "norm_b": norm_b, "norm_c": norm_c,
            "w_out": w_out, "ln_g": ln_g, "ln_b": ln_b}


def reference(x, w_in, conv_a_w, sinks, conv_c_w, conv_c_b, gate_r_w, gate_r_b,
              gate_i_w, gate_i_b, rg_lambda, norm_a, norm_b, norm_c, w_out, ln_g, ln_b):
    for l in range(DEPTH):
        x = hybrid_layer(x, w_in[l], conv_a_w[l], sinks[l], conv_c_w[l], conv_c_b[l],
                         gate_r_w[l], gate_r_b[l], gate_i_w[l], gate_i_b[l], rg_lambda[l],
                         norm_a[l], norm_b[l], norm_c[l], w_out[l], ln_g[l], ln_b[l])
    return x
```

```python
import functools

import jax
import jax.numpy as jnp
from jax import lax
from jax.experimental import pallas as pl
from jax.experimental.pallas import tpu as pltpu

D_MODEL = 4096
DEPTH = 4
W_A = D_MODEL // 4
W_B = D_MODEL // 2
W_C = D_MODEL // 4
D_MIX = W_A + W_B + W_C
HEAD_DIM = 64
N_Q_HEADS = W_B // HEAD_DIM
KV_GROUP = 8
N_KV_HEADS = N_Q_HEADS // KV_GROUP
KV_W = N_KV_HEADS * HEAD_DIM
WINDOW = 128
BLOCK = 128
CONV_A = 3
CONV_C = 4
N_RG_HEADS = 8
RG_BLOCK = W_C // N_RG_HEADS
RG_C = 8.0
IN_W = 4 * W_A + (W_B + 2 * KV_W + W_B) + 2 * W_C
DEEPNORM_ALPHA = (2.0 * DEPTH) ** 0.25
LN_EPS = 1e-5
RMS_EPS = 1e-6
NEG_INF = -1e30

OFF_AB = 0
OFF_AC = W_A
OFF_AX = 2 * W_A
OFF_AG = 3 * W_A
OFF_Q = 4 * W_A
OFF_K = OFF_Q + W_B
OFF_V = OFF_K + KV_W
OFF_BG = OFF_V + KV_W
OFF_CX = OFF_BG + W_B
OFF_CG = OFF_CX + W_C

LANES = 128
SUBLANES = 8
VMEM_LIMIT = 56 * 1024 * 1024

PAIRS = KV_GROUP // 2
QW = KV_GROUP * HEAD_DIM

BF16 = jnp.bfloat16
F32 = jnp.float32


def _matmul_kernel(x_ref, w_ref, o_ref):
    o_ref[...] = jnp.dot(x_ref[...], w_ref[...], preferred_element_type=F32).astype(o_ref.dtype)


def _in_proj(xb, w, tm=1024, tn=1536):
    m, k = xb.shape
    n = w.shape[1]
    return pl.pallas_call(
        _matmul_kernel,
        grid=(n // tn, m // tm),
        in_specs=[pl.BlockSpec((tm, k), lambda j, i: (i, 0)),
                  pl.BlockSpec((k, tn), lambda j, i: (0, j))],
        out_specs=pl.BlockSpec((tm, tn), lambda j, i: (i, j)),
        out_shape=jax.ShapeDtypeStruct((m, n), BF16),
        compiler_params=pltpu.CompilerParams(
            dimension_semantics=("arbitrary", "arbitrary"), vmem_limit_bytes=VMEM_LIMIT),
        name="in_proj",
    )(xb, w)


def _out_proj_kernel(mix_ref, w_ref, x_ref, y_ref):
    acc = jnp.dot(mix_ref[...], w_ref[...], preferred_element_type=F32)
    y_ref[...] = DEEPNORM_ALPHA * x_ref[...] + acc


def _out_proj(mix, w, x, tm=1024, tn=512):
    m, k = mix.shape
    n = w.shape[1]
    return pl.pallas_call(
        _out_proj_kernel,
        grid=(n // tn, m // tm),
        in_specs=[pl.BlockSpec((tm, k), lambda j, i: (i, 0)),
                  pl.BlockSpec((k, tn), lambda j, i: (0, j)),
                  pl.BlockSpec((tm, tn), lambda j, i: (i, j))],
        out_specs=pl.BlockSpec((tm, tn), lambda j, i: (i, j)),
        out_shape=jax.ShapeDtypeStruct((m, n), F32),
        compiler_params=pltpu.CompilerParams(
            dimension_semantics=("arbitrary", "arbitrary"), vmem_limit_bytes=VMEM_LIMIT),
        name="out_proj",
    )(mix, w, x)


def _ln_kernel(y_ref, g_ref, b_ref, x_ref, xb_ref):
    y = y_ref[...]
    mu = jnp.mean(y, axis=-1, keepdims=True)
    d = y - mu
    var = jnp.mean(d * d, axis=-1, keepdims=True)
    x = d * lax.rsqrt(var + LN_EPS) * g_ref[...] + b_ref[...]
    x_ref[...] = x
    xb_ref[...] = x.astype(BF16)


def _layer_norm(y, g, b, tm=256):
    m, d = y.shape
    return pl.pallas_call(
        _ln_kernel,
        grid=(m // tm,),
        in_specs=[pl.BlockSpec((tm, d), lambda i: (i, 0)),
                  pl.BlockSpec((1, d), lambda i: (0, 0)),
                  pl.BlockSpec((1, d), lambda i: (0, 0))],
        out_specs=[pl.BlockSpec((tm, d), lambda i: (i, 0)),
                   pl.BlockSpec((tm, d), lambda i: (i, 0))],
        out_shape=[jax.ShapeDtypeStruct((m, d), F32), jax.ShapeDtypeStruct((m, d), BF16)],
        compiler_params=pltpu.CompilerParams(
            dimension_semantics=("arbitrary",), vmem_limit_bytes=VMEM_LIMIT),
        name="layer_norm",
    )(y, g.reshape(1, d), b.reshape(1, d))


def _silu(g):
    return g * jax.nn.sigmoid(g)


def _rms_gate(y, gain, gate):
    ms = jnp.mean(y * y, axis=-1, keepdims=True)
    return y * lax.rsqrt(ms + RMS_EPS) * gain * _silu(gate)


def _mixer_kernel(sinks_ref, h_ref, caw_ref, ccw_ref, ccb_ref, wg_ref, br_ref, bi_ref, lam_ref,
                  na_ref, nb_ref, nc_ref, o_ref,
                  pa_ext, cx_ext, kprev, vprev, hprev, yb_scr):
    T = BLOCK
    n = pl.program_id(1)

    @pl.when(n == 0)
    def _():
        pa_ext[0:SUBLANES, :] = jnp.zeros((SUBLANES, W_A), F32)
        cx_ext[0:SUBLANES, :] = jnp.zeros((SUBLANES, W_C), F32)
        kprev[...] = jnp.zeros((T, KV_W), F32)
        vprev[...] = jnp.zeros((T, KV_W), F32)
        hprev[...] = jnp.zeros((SUBLANES, W_C), F32)

    def col(off, width):
        return h_ref[:, off:off + width]

    a_b = col(OFF_AB, W_A).astype(F32)
    pa = col(OFF_AC, W_A).astype(F32) * col(OFF_AX, W_A).astype(F32)
    pa_ext[SUBLANES:SUBLANES + T, :] = pa
    conv_a = (caw_ref[0:1, :] * pa_ext[SUBLANES - 2:SUBLANES - 2 + T, :]
              + caw_ref[1:2, :] * pa_ext[SUBLANES - 1:SUBLANES - 1 + T, :]
              + caw_ref[2:3, :] * pa)
    pa_ext[0:SUBLANES, :] = pa_ext[T:T + SUBLANES, :]
    y_a = a_b * conv_a
    o_ref[:, 0:W_A] = _rms_gate(y_a, na_ref[...], col(OFF_AG, W_A).astype(F32)).astype(BF16)

    cx = col(OFF_CX, W_C).astype(F32)
    cx_ext[SUBLANES:SUBLANES + T, :] = cx
    xc = (ccw_ref[0:1, :] * cx_ext[SUBLANES - 3:SUBLANES - 3 + T, :]
          + ccw_ref[1:2, :] * cx_ext[SUBLANES - 2:SUBLANES - 2 + T, :]
          + ccw_ref[2:3, :] * cx_ext[SUBLANES - 1:SUBLANES - 1 + T, :]
          + ccw_ref[3:4, :] * cx) + ccb_ref[...]
    cx_ext[0:SUBLANES, :] = cx_ext[T:T + SUBLANES, :]
    xcb = xc.astype(BF16)
    gates = [jnp.dot(xcb[:, hh * RG_BLOCK:(hh + 1) * RG_BLOCK], wg_ref[hh],
                     preferred_element_type=F32) for hh in range(N_RG_HEADS)]
    g_r = jnp.concatenate([g[:, :RG_BLOCK] for g in gates], axis=1) + br_ref[...]
    g_i = jnp.concatenate([g[:, RG_BLOCK:] for g in gates], axis=1) + bi_ref[...]
    r = jax.nn.sigmoid(g_r)
    ig = jax.nn.sigmoid(g_i)
    lam = lam_ref[...]
    log_sig = jnp.minimum(lam, 0.0) - jnp.log1p(jnp.exp(-jnp.abs(lam)))
    log_a = RG_C * r * log_sig
    a = jnp.exp(log_a)
    th = jnp.tanh(log_a)
    u = jnp.sqrt(-2.0 * th / (1.0 - th)) * (ig * xc)
    row = lax.broadcasted_iota(jnp.int32, (T, W_C), 0)
    u = u + jnp.where(row == 0, a * hprev[SUBLANES - 1:SUBLANES, :], 0.0)
    s = 1
    while s < T:
        keep = row >= s
        a_sh = jnp.where(keep, pltpu.roll(a, s, 0), 1.0)
        u_sh = jnp.where(keep, pltpu.roll(u, s, 0), 0.0)
        u = a * u_sh + u
        a = a * a_sh
        s *= 2
    hprev[...] = u[T - SUBLANES:T, :]
    o_ref[:, W_A + W_B:D_MIX] = _rms_gate(u, nc_ref[...], col(OFF_CG, W_C).astype(F32)).astype(BF16)

    k_cur = col(OFF_K, KV_W).astype(F32)
    v_cur = col(OFF_V, KV_W).astype(F32)
    kk = jnp.concatenate([kprev[...], k_cur], axis=0)
    vv = jnp.concatenate([vprev[...], v_cur], axis=0)
    kprev[...] = k_cur
    vprev[...] = v_cur

    lane = lax.broadcasted_iota(jnp.int32, (2 * T, LANES), 1)
    lo = lane < HEAD_DIM
    ri = lax.broadcasted_iota(jnp.int32, (PAIRS * T, 4 * T), 0) % T
    ci = lax.broadcasted_iota(jnp.int32, (PAIRS * T, 4 * T), 1) % (2 * T)
    dist = ci - ri
    first_key = jnp.where(n == 0, T, 0)
    valid = (dist >= 1) & (dist <= WINDOW) & (ci >= first_key)
    out_lane_lo = lax.broadcasted_iota(jnp.int32, (PAIRS * T, LANES), 1) < HEAD_DIM

    def split_heads(xx, hh):
        c = hh // 2
        blk = xx[:, c * LANES:(c + 1) * LANES]
        rolled = pltpu.roll(blk, HEAD_DIM, 1)
        if hh % 2 == 0:
            even, odd = blk, rolled
        else:
            even, odd = rolled, blk
        return jnp.concatenate([jnp.where(lo, even, 0.0), jnp.where(lo, 0.0, odd)], axis=0).astype(BF16)

    for hh in range(N_KV_HEADS):
        q_h = jnp.concatenate(
            [h_ref[:, OFF_Q + hh * QW + p * LANES:OFF_Q + hh * QW + (p + 1) * LANES] for p in range(PAIRS)],
            axis=0) * (HEAD_DIM ** -0.5)
        w_k = split_heads(kk, hh)
        w_v = split_heads(vv, hh)
        sc = lax.dot_general(q_h.astype(BF16), w_k, (((1,), (1,)), ((), ())), preferred_element_type=F32)
        sc = jnp.where(valid, sc, NEG_INF)
        sink_e = jnp.concatenate(
            [jnp.full((T, 1), sinks_ref[hh * KV_GROUP + 2 * p], F32) for p in range(PAIRS)], axis=0)
        sink_o = jnp.concatenate(
            [jnp.full((T, 1), sinks_ref[hh * KV_GROUP + 2 * p + 1], F32) for p in range(PAIRS)], axis=0)
        s_e = sc[:, :2 * T]
        s_o = sc[:, 2 * T:]
        m_e = jnp.maximum(jnp.max(s_e, axis=-1, keepdims=True), sink_e)
        m_o = jnp.maximum(jnp.max(s_o, axis=-1, keepdims=True), sink_o)
        p_e = jnp.exp(s_e - m_e)
        p_o = jnp.exp(s_o - m_o)
        l_e = jnp.sum(p_e, axis=-1, keepdims=True) + jnp.exp(sink_e - m_e)
        l_o = jnp.sum(p_o, axis=-1, keepdims=True) + jnp.exp(sink_o - m_o)
        pcat = jnp.concatenate([p_e.astype(BF16), p_o.astype(BF16)], axis=1)
        o_h = jnp.dot(pcat, w_v, preferred_element_type=F32)
        o_h = o_h / jnp.where(out_lane_lo, l_e, l_o)
        for p in range(PAIRS):
            yb_scr[:, hh * QW + p * LANES:hh * QW + (p + 1) * LANES] = o_h[p * T:(p + 1) * T, :]

    o_ref[:, W_A:W_A + W_B] = _rms_gate(yb_scr[...], nb_ref[...], col(OFF_BG, W_B).astype(F32)).astype(BF16)


def _mixer(h, sinks, conv_a_w, conv_c_w, conv_c_b, w_gate, b_r, b_i, lam, norm_a, norm_b, norm_c, batch, seq):
    T = BLOCK
    nb = seq // T
    row = lambda a: a.reshape(1, -1)
    full2 = lambda shape: pl.BlockSpec(shape, lambda b, n: (0, 0))
    return pl.pallas_call(
        _mixer_kernel,
        grid=(batch, nb),
        in_specs=[pl.BlockSpec(memory_space=pltpu.SMEM),
                  pl.BlockSpec((T, IN_W), lambda b, n: (b * nb + n, 0)),
                  full2((CONV_A, W_A)), full2((CONV_C, W_C)), full2((1, W_C)),
                  pl.BlockSpec((N_RG_HEADS, RG_BLOCK, 2 * RG_BLOCK), lambda b, n: (0, 0, 0)),
                  full2((1, W_C)), full2((1, W_C)), full2((1, W_C)),
                  full2((1, W_A)), full2((1, W_B)), full2((1, W_C))],
        out_specs=pl.BlockSpec((T, D_MIX), lambda b, n: (b * nb + n, 0)),
        out_shape=jax.ShapeDtypeStruct((batch * seq, D_MIX), BF16),
        scratch_shapes=[pltpu.VMEM((T + 2 * SUBLANES, W_A), F32),
                        pltpu.VMEM((T + 2 * SUBLANES, W_C), F32),
                        pltpu.VMEM((T, KV_W), F32),
                        pltpu.VMEM((T, KV_W), F32),
                        pltpu.VMEM((SUBLANES, W_C), F32),
                        pltpu.VMEM((T, W_B), F32)],
        compiler_params=pltpu.CompilerParams(
            dimension_semantics=("arbitrary", "arbitrary"), vmem_limit_bytes=VMEM_LIMIT),
        name="mixer",
    )(sinks, h, conv_a_w, conv_c_w, row(conv_c_b), w_gate, row(b_r), row(b_i), row(lam),
      row(norm_a), row(norm_b), row(norm_c))


def kernel(x, w_in, conv_a_w, sinks, conv_c_w, conv_c_b, gate_r_w, gate_r_b, gate_i_w, gate_i_b, rg_lambda,
           norm_a, norm_b, norm_c, w_out, ln_g, ln_b):
    batch, seq, d = x.shape
    m = batch * seq
    xf = x.reshape(m, d)
    xb = xf.astype(BF16)
    w_in_b = w_in.astype(BF16)
    w_out_b = w_out.astype(BF16)
    w_gate = jnp.concatenate([gate_r_w, gate_i_w], axis=-1).astype(BF16)
    for l in range(DEPTH):
        h = _in_proj(xb, w_in_b[l])
        mix = _mixer(h, sinks[l], conv_a_w[l], conv_c_w[l], conv_c_b[l], w_gate[l], gate_r_b[l], gate_i_b[l],
                     rg_lambda[l], norm_a[l], norm_b[l], norm_c[l], batch, seq)
        y = _out_proj(mix, w_out_b[l], xf)
        xf, xb = _layer_norm(y, ln_g[l], ln_b[l])
    return xf.reshape(batch, seq, d)
```
